```python
import jax, jax.numpy as jnp
from jax import lax
import numpy as np

D_MODEL = 1024
BATCH = 8
SEQ = 2048
DEPTH = 1
DEC_BATCH = 128
DEC_SEQ = 4
PAST_LEN = 16384
PAGE_SIZE = 128

MIX_WIDTH = D_MODEL
CONV_CH = MIX_WIDTH // 2
CONV_GROUPS = 8
MLP_CH = MIX_WIDTH - CONV_CH
MLP_HEADS = 8
MLP_HEAD_DIM = MLP_CH // MLP_HEADS
CONV_K = 3
CHUNK = 128
EPS = 1e-6
PROJ_WIDTH = 4 * CONV_CH + 3 * MLP_CH

kernel_name = "hybrid_shortconv_chunkmlp_adaln_step"


def rms_norm(x, g):
    xf = x.astype(jnp.float32)
    out = xf * lax.rsqrt(jnp.mean(xf * xf, axis=-1, keepdims=True) + EPS) * g.astype(jnp.float32)
    return out.astype(x.dtype)


def layer_norm(x, g, b):
    xf = x.astype(jnp.float32)
    mu = jnp.mean(xf, axis=-1, keepdims=True)
    xc = xf - mu
    var = jnp.mean(xc * xc, axis=-1, keepdims=True)
    out = xc * lax.rsqrt(var + EPS) * g.astype(jnp.float32) + b.astype(jnp.float32)
    return out.astype(x.dtype)


def chunk_spatial_mix(v, w_s, b_s):
    bsz, t, _ = v.shape
    n_chunks = -(-t // CHUNK)
    pad = n_chunks * CHUNK - t
    vp = jnp.pad(v, ((0, 0), (0, pad), (0, 0))).reshape(bsz, n_chunks, CHUNK, MLP_HEADS, MLP_HEAD_DIM)
    mask = jnp.tril(jnp.ones((CHUNK, CHUNK), dtype=bool))
    w = jnp.where(mask[None], w_s, jnp.zeros((), w_s.dtype))
    out = jnp.einsum("hts,bnshd->bnthd", w, vp) + b_s.T[None, None, :, :, None]
    return out.reshape(bsz, n_chunks * CHUNK, MLP_CH)[:, :t]


def mixer_layer(x, c, conv_buf, w_ada, b_ada, g_norm, w_in, conv_w, g_v, b_v, w_s, b_s, w_out):
    t = x.shape[1]
    mod = jax.nn.silu(c) @ w_ada + b_ada
    shift, scale, gate = jnp.split(mod[:, None, :], 3, axis=-1)
    h = rms_norm(x, g_norm) * (1 + scale) + shift
    p = h @ w_in
    splits = [CONV_CH, 2 * CONV_CH, 3 * CONV_CH, 4 * CONV_CH,
              4 * CONV_CH + MLP_CH, 4 * CONV_CH + 2 * MLP_CH]
    h_a, b_a, c_a, z_a, u_b, v_b, z_b = jnp.split(p, splits, axis=-1)
    s = c_a * h_a
    s_pad = jnp.concatenate([conv_buf.astype(s.dtype), s], axis=1)
    conv = conv_w[0] * s_pad[:, 0:t]
    for k in range(1, CONV_K):
        conv = conv + conv_w[k] * s_pad[:, k:k + t]
    y_a = b_a * conv * jax.nn.silu(z_a)
    new_buf = s_pad[:, t:]
    v_n = layer_norm(v_b, g_v, b_v)
    y_b = u_b * chunk_spatial_mix(v_n, w_s, b_s) * jax.nn.silu(z_b)
    y = jnp.concatenate([y_a, y_b], axis=-1) @ w_out
    return x + gate * y, new_buf, v_n


def setup_inputs(seed: int = 0) -> dict:
    key = jax.random.key(seed)
    ks = jax.random.split(key, 20)
    f = jnp.float32
    nrm = lambda k, shape: jax.random.normal(k, shape, f)
    inputs = {
        "x_prompt": nrm(ks[0], (BATCH, SEQ, D_MODEL)),
        "x_sample": nrm(ks[1], (DEC_BATCH, DEC_SEQ, D_MODEL)),
        "state_conv": nrm(ks[2], (DEPTH, DEC_BATCH, CONV_K - 1, CONV_CH)),
        "c_prompt": nrm(ks[3], (BATCH, D_MODEL)),
        "c_sample": nrm(ks[4], (DEC_BATCH, D_MODEL)),
        "w_ada": nrm(ks[5], (DEPTH, D_MODEL, 3 * D_MODEL)) * (0.5 * D_MODEL ** -0.5),
        "b_ada": nrm(ks[6], (DEPTH, 3 * D_MODEL)) * 0.02,
        "g_norm": 1.0 + 0.02 * nrm(ks[7], (DEPTH, D_MODEL)),
        "w_in": nrm(ks[8], (DEPTH, D_MODEL, PROJ_WIDTH)) * D_MODEL ** -0.5,
        "conv_w": nrm(ks[9], (DEPTH, CONV_K, CONV_CH)) * CONV_K ** -0.5,
        "g_v": 1.0 + 0.02 * nrm(ks[10], (DEPTH, MLP_CH)),
        "b_v": 0.02 * nrm(ks[11], (DEPTH, MLP_CH)),
        "w_s": nrm(ks[12], (DEPTH, MLP_HEADS, CHUNK, CHUNK)) * CHUNK ** -0.5,
        "b_s": 1.0 + 0.02 * nrm(ks[13], (DEPTH, MLP_HEADS, CHUNK)),
        "w_out": nrm(ks[14], (DEPTH, MIX_WIDTH, D_MODEL)) * MIX_WIDTH ** -0.5,
        "g_final": 1.0 + 0.02 * nrm(ks[15], (D_MODEL,)),
    }
    return inputs


def reference(x_prompt, x_sample, state_conv, c_prompt, c_sample, w_ada, b_ada, g_norm, w_in,
              conv_w, g_v, b_v, w_s, b_s, w_out, g_final):
    hp = x_prompt
    hs = x_sample
    conv_p, conv_s, v_s = [], [], []
    zero_buf = jnp.zeros((x_prompt.shape[0], CONV_K - 1, CONV_CH), x_prompt.dtype)
    for l in range(DEPTH):
        hp, buf_p, _ = mixer_layer(hp, c_prompt, zero_buf, w_ada[l], b_ada[l], g_norm[l], w_in[l],
                                   conv_w[l], g_v[l], b_v[l], w_s[l], b_s[l], w_out[l])
        hs, buf_s, vn_s = mixer_layer(hs, c_sample, state_conv[l], w_ada[l], b_ada[l], g_norm[l], w_in[l],
                                      conv_w[l], g_v[l], b_v[l], w_s[l], b_s[l], w_out[l])
        conv_p.append(buf_p)
        conv_s.append(buf_s)
        v_s.append(vn_s)
    y_prompt = rms_norm(hp, g_final)
    y_sample = rms_norm(hs, g_final)
    new_conv_prompt = jnp.stack(conv_p)
    new_conv_sample = jnp.stack(conv_s)
    new_v_sample = jnp.stack(v_s)
    return (y_prompt, y_sample, new_conv_prompt, new_conv_sample, new_v_sample)
```

```python
import functools

import jax
import jax.numpy as jnp
from jax import lax
from jax.experimental import pallas as pl
from jax.experimental.pallas import tpu as pltpu

D_MODEL = 1024
CONV_CH = 512
MLP_CH = 512
MLP_HEADS = 8
MLP_HEAD_DIM = 64
CONV_K = 3
CHUNK = 128
EPS = 1e-6
PROJ_WIDTH = 4 * CONV_CH + 3 * MLP_CH
HEAD_PAIRS = MLP_HEADS // 2

V7X_LANES = 128
V7X_SUBLANES = 8
V7X_VMEM_LIMIT_BYTES = 56 * 1024 * 1024

ROW_TILE = 512
ADA_COL_TILE = 512

F32 = jnp.float32
BF16 = jnp.bfloat16


def _silu(z):
    return z * jax.nn.sigmoid(z)


def _rms_scale(x):
    return lax.rsqrt(jnp.mean(x * x, axis=-1, keepdims=True) + EPS)


def _modulated_norm(x, g_norm, shift, scale):
    return x * _rms_scale(x) * g_norm * (1.0 + scale) + shift


def _layer_norm(v, g, b):
    mu = jnp.mean(v, axis=-1, keepdims=True)
    vc = v - mu
    var = jnp.mean(vc * vc, axis=-1, keepdims=True)
    return vc * lax.rsqrt(var + EPS) * g + b


def _split_proj(p):
    return [p[:, i * CONV_CH:(i + 1) * CONV_CH] for i in range(7)]


def _adaln_kernel(cp_ref, cs_ref, w_ref, b_ref, mp_ref, ms_ref):
    w = w_ref[...].astype(BF16)
    b = b_ref[...]
    ap = _silu(cp_ref[...]).astype(BF16)
    as_ = _silu(cs_ref[...]).astype(BF16)
    mp_ref[...] = jnp.dot(ap, w, preferred_element_type=F32) + b
    ms_ref[...] = jnp.dot(as_, w, preferred_element_type=F32) + b


def _adaln(c_prompt, c_sample, w_ada, b_ada):
    nb_p, nb_s = c_prompt.shape[0], c_sample.shape[0]
    n_out = w_ada.shape[1]
    return pl.pallas_call(
        _adaln_kernel,
        grid=(n_out // ADA_COL_TILE,),
        in_specs=[
            pl.BlockSpec((nb_p, D_MODEL), lambda j: (0, 0)),
            pl.BlockSpec((nb_s, D_MODEL), lambda j: (0, 0)),
            pl.BlockSpec((D_MODEL, ADA_COL_TILE), lambda j: (0, j)),
            pl.BlockSpec((1, ADA_COL_TILE), lambda j: (0, j)),
        ],
        out_specs=[
            pl.BlockSpec((nb_p, ADA_COL_TILE), lambda j: (0, j)),
            pl.BlockSpec((nb_s, ADA_COL_TILE), lambda j: (0, j)),
        ],
        out_shape=[
            jax.ShapeDtypeStruct((nb_p, n_out), F32),
            jax.ShapeDtypeStruct((nb_s, n_out), F32),
        ],
        name="adaln_mod",
    )(c_prompt, c_sample, w_ada, b_ada.reshape(1, n_out))


def _paired_tril_weights(ws_ref):
    row = lax.broadcasted_iota(jnp.int32, (CHUNK, CHUNK), 0)
    col = lax.broadcasted_iota(jnp.int32, (CHUNK, CHUNK), 1)
    causal = col <= row
    out = []
    for k in range(HEAD_PAIRS):
        w0 = jnp.where(causal, ws_ref[2 * k], 0.0)
        w1 = jnp.where(causal, ws_ref[2 * k + 1], 0.0)
        out.append(jnp.concatenate([w0, w1], axis=1).astype(BF16))
    return out


def _chunk_mix(vn, wcat, n_chunks):
    vb = vn.astype(BF16)
    lane = lax.broadcasted_iota(jnp.int32, (CHUNK, V7X_LANES), 1)
    low_half = lane < MLP_HEAD_DIM
    zero = jnp.zeros((), BF16)
    pair_out = []
    for k in range(HEAD_PAIRS):
        lo, hi = [], []
        for n in range(n_chunks):
            blk = vb[n * CHUNK:(n + 1) * CHUNK, k * V7X_LANES:(k + 1) * V7X_LANES]
            lo.append(jnp.where(low_half, blk, zero))
            hi.append(jnp.where(low_half, zero, blk))
        rhs = jnp.concatenate(
            [jnp.concatenate(lo, axis=1), jnp.concatenate(hi, axis=1)], axis=0)
        pair_out.append(jnp.dot(wcat[k], rhs, preferred_element_type=F32))
    rows = []
    for n in range(n_chunks):
        rows.append(jnp.concatenate(
            [pair_out[k][:, n * V7X_LANES:(n + 1) * V7X_LANES] for k in range(HEAD_PAIRS)], axis=1))
    return rows


def _prompt_kernel(x_ref, mod_ref, gn_ref, win_ref, cw_ref, gv_ref, bv_ref, ws_ref, bs_ref,
                   wout_ref, gf_ref, y_ref, cbuf_ref, spad_ref):
    tm = x_ref.shape[1]
    n_chunks = tm // CHUNK
    j = pl.program_id(1)

    x = x_ref[0]
    mod = mod_ref[0]
    shift = mod[:, :D_MODEL]
    scale = mod[:, D_MODEL:2 * D_MODEL]
    gate = mod[:, 2 * D_MODEL:]

    h = _modulated_norm(x, gn_ref[...], shift, scale).astype(BF16)
    p = jnp.dot(h, win_ref[...], preferred_element_type=F32)
    h_a, b_a, c_a, z_a, u_b, v_b, z_b = _split_proj(p)

    s = c_a * h_a

    @pl.when(j == 0)
    def _():
        spad_ref[0:V7X_SUBLANES, :] = jnp.zeros((V7X_SUBLANES, CONV_CH), F32)

    spad_ref[V7X_SUBLANES:V7X_SUBLANES + tm, :] = s
    s_m1 = spad_ref[V7X_SUBLANES - 1:V7X_SUBLANES - 1 + tm, :]
    s_m2 = spad_ref[V7X_SUBLANES - 2:V7X_SUBLANES - 2 + tm, :]
    cw = cw_ref[...]
    conv = cw[0:1] * s_m2 + cw[1:2] * s_m1 + cw[2:3] * s
    y_a = b_a * conv * _silu(z_a)
    tail = spad_ref[tm:tm + V7X_SUBLANES, :]
    spad_ref[0:V7X_SUBLANES, :] = tail
    cbuf_ref[0] = tail[V7X_SUBLANES - (CONV_K - 1):, :]

    vn = _layer_norm(v_b, gv_ref[...], bv_ref[...])
    mix_rows = _chunk_mix(vn, _paired_tril_weights(ws_ref), n_chunks)
    bias = bs_ref[...]
    mix = jnp.concatenate([m + bias for m in mix_rows], axis=0)
    y_b = u_b * mix * _silu(z_b)

    y_cat = jnp.concatenate([y_a, y_b], axis=1).astype(BF16)
    y = jnp.dot(y_cat, wout_ref[...], preferred_element_type=F32)
    res = x + gate * y
    y_ref[0] = res * _rms_scale(res) * gf_ref[...]


def _prompt_layer(x, mod, g_norm, w_in_b, conv_w, g_v, b_v, w_s, bias_rows, w_out_b, g_final):
    nb, seq, _ = x.shape
    tm = ROW_TILE
    const2 = lambda b, j: (0, 0)
    return pl.pallas_call(
        _prompt_kernel,
        grid=(nb, seq // tm),
        in_specs=[
            pl.BlockSpec((1, tm, D_MODEL), lambda b, j: (b, j, 0)),
            pl.BlockSpec((1, 1, 3 * D_MODEL), lambda b, j: (b, 0, 0)),
            pl.BlockSpec((1, D_MODEL), const2),
            pl.BlockSpec((D_MODEL, PROJ_WIDTH), const2),
            pl.BlockSpec((CONV_K, CONV_CH), const2),
            pl.BlockSpec((1, MLP_CH), const2),
            pl.BlockSpec((1, MLP_CH), const2),
            pl.BlockSpec((MLP_HEADS, CHUNK, CHUNK), lambda b, j: (0, 0, 0)),
            pl.BlockSpec((CHUNK, MLP_CH), const2),
            pl.BlockSpec((D_MODEL, D_MODEL), const2),
            pl.BlockSpec((1, D_MODEL), const2),
        ],
        out_specs=[
            pl.BlockSpec((1, tm, D_MODEL), lambda b, j: (b, j, 0)),
            pl.BlockSpec((1, CONV_K - 1, CONV_CH), lambda b, j: (b, 0, 0)),
        ],
        out_shape=[
            jax.ShapeDtypeStruct((nb, seq, D_MODEL), F32),
            jax.ShapeDtypeStruct((nb, CONV_K - 1, CONV_CH), F32),
        ],
        scratch_shapes=[pltpu.VMEM((tm + V7X_SUBLANES, CONV_CH), F32)],
        compiler_params=pltpu.CompilerParams(
            dimension_semantics=("arbitrary", "arbitrary"),
            vmem_limit_bytes=V7X_VMEM_LIMIT_BYTES),
        name="prompt_layer",
    )(x, mod.reshape(nb, 1, 3 * D_MODEL), g_norm.reshape(1, D_MODEL), w_in_b, conv_w,
      g_v.reshape(1, MLP_CH), b_v.reshape(1, MLP_CH), w_s, bias_rows, w_out_b,
      g_final.reshape(1, D_MODEL))


def _sample_kernel(x_ref, st_ref, mod_ref, gn_ref, win_ref, cw_ref, gv_ref, bv_ref, wv_ref, bs_ref,
                   wout_ref, gf_ref, y_ref, cbuf_ref, vn_ref):
    nb = x_ref.shape[0]
    n_tok = x_ref.shape[1] // D_MODEL
    mod = mod_ref[...]
    shift = mod[:, :D_MODEL]
    scale = mod[:, D_MODEL:2 * D_MODEL]
    gate = mod[:, 2 * D_MODEL:]
    gn = gn_ref[...]

    xs = [x_ref[:, t * D_MODEL:(t + 1) * D_MODEL] for t in range(n_tok)]
    h = jnp.concatenate([_modulated_norm(x, gn, shift, scale).astype(BF16) for x in xs], axis=0)
    p = jnp.dot(h, win_ref[...], preferred_element_type=F32)

    cw = cw_ref[...]
    gv, bv = gv_ref[...], bv_ref[...]
    wv = wv_ref[...]
    bias = bs_ref[...]
    s_pad = [st_ref[:, k * CONV_CH:(k + 1) * CONV_CH] for k in range(CONV_K - 1)]
    vns, ys = [], []
    for t in range(n_tok):
        h_a, b_a, c_a, z_a, u_b, v_b, z_b = _split_proj(p[t * nb:(t + 1) * nb])
        s_pad.append(c_a * h_a)
        conv = cw[0:1] * s_pad[t] + cw[1:2] * s_pad[t + 1] + cw[2:3] * s_pad[t + 2]
        y_a = b_a * conv * _silu(z_a)
        vns.append(_layer_norm(v_b, gv, bv))
        mix = bias[t:t + 1]
        for src in range(t + 1):
            r = t * n_tok + src
            mix = mix + wv[r:r + 1] * vns[src]
        y_b = u_b * mix * _silu(z_b)
        ys.append(jnp.concatenate([y_a, y_b], axis=1).astype(BF16))

    y = jnp.dot(jnp.concatenate(ys, axis=0), wout_ref[...], preferred_element_type=F32)
    gf = gf_ref[...]
    for t in range(n_tok):
        res = xs[t] + gate * y[t * nb:(t + 1) * nb]
        y_ref[:, t * D_MODEL:(t + 1) * D_MODEL] = res * _rms_scale(res) * gf
        vn_ref[:, t * MLP_CH:(t + 1) * MLP_CH] = vns[t]
    for k in range(CONV_K - 1):
        cbuf_ref[:, k * CONV_CH:(k + 1) * CONV_CH] = s_pad[n_tok + k]


def _sample_layer(x2d, st2d, mod, g_norm, w_in_b, conv_w, g_v, b_v, wv_rows, bias_rows, w_out_b,
                  g_final):
    nb = x2d.shape[0]
    n_tok = x2d.shape[1] // D_MODEL
    full = lambda a: pl.BlockSpec(a.shape, lambda i: (0,) * a.ndim)
    args = (x2d, st2d, mod, g_norm.reshape(1, D_MODEL), w_in_b, conv_w, g_v.reshape(1, MLP_CH),
            b_v.reshape(1, MLP_CH), wv_rows, bias_rows, w_out_b, g_final.reshape(1, D_MODEL))
    out_shape = [
        jax.ShapeDtypeStruct((nb, n_tok * D_MODEL), F32),
        jax.ShapeDtypeStruct((nb, (CONV_K - 1) * CONV_CH), F32),
        jax.ShapeDtypeStruct((nb, n_tok * MLP_CH), F32),
    ]
    return pl.pallas_call(
        _sample_kernel,
        grid=(1,),
        in_specs=[full(a) for a in args],
        out_specs=[pl.BlockSpec(s.shape, lambda i: (0, 0)) for s in out_shape],
        out_shape=out_shape,
        compiler_params=pltpu.CompilerParams(
            dimension_semantics=("arbitrary",),
            vmem_limit_bytes=V7X_VMEM_LIMIT_BYTES),
        name="sample_layer",
    )(*args)


def kernel(x_prompt, x_sample, state_conv, c_prompt, c_sample, w_ada, b_ada, g_norm, w_in, conv_w,
           g_v, b_v, w_s, b_s, w_out, g_final):
    depth = w_ada.shape[0]
    assert depth == 1, "single-layer trunk"
    nb_s, n_tok, _ = x_sample.shape
    assert n_tok <= CHUNK

    mod_p, mod_s = _adaln(c_prompt, c_sample, w_ada[0], b_ada[0])
    w_in_b = w_in[0].astype(BF16)
    w_out_b = w_out[0].astype(BF16)
    bias_rows = jnp.repeat(b_s[0].T, MLP_HEAD_DIM, axis=1)
    wv_rows = jnp.repeat(
        w_s[0][:, :n_tok, :n_tok].transpose(1, 2, 0).reshape(n_tok * n_tok, MLP_HEADS),
        MLP_HEAD_DIM, axis=1)

    y_prompt, conv_p = _prompt_layer(x_prompt, mod_p, g_norm[0], w_in_b, conv_w[0], g_v[0], b_v[0],
                                     w_s[0], bias_rows, w_out_b, g_final)
    y_s, conv_s, vn_s = _sample_layer(
        x_sample.reshape(nb_s, n_tok * D_MODEL),
        state_conv[0].reshape(nb_s, (CONV_K - 1) * CONV_CH),
        mod_s, g_norm[0], w_in_b, conv_w[0], g_v[0], b_v[0], wv_rows, bias_rows, w_out_b, g_final)

    return (y_prompt,
            y_s.reshape(nb_s, n_tok, D_MODEL),
            conv_p[None],
            conv_s.reshape(1, nb_s, CONV_K - 1, CONV_CH),
            vn_s.reshape(1, nb_s, n_tok, MLP_CH))
```

```python
import functools

import jax
import jax.numpy as jnp
from jax import lax
from jax.experimental import pallas as pl
from jax.experimental.pallas import tpu as pltpu

D_MODEL = 1024
CONV_CH = 512
MLP_CH = 512
MLP_HEADS = 8
MLP_HEAD_DIM = 64
CONV_K = 3
CHUNK = 128
EPS = 1e-6
PROJ_WIDTH = 4 * CONV_CH + 3 * MLP_CH
HEAD_PAIRS = MLP_HEADS // 2

V7X_LANES = 128
V7X_SUBLANES = 8
V7X_VMEM_LIMIT_BYTES = 56 * 1024 * 1024

ROW_TILE = 512
SUB_TILE = 256
ADA_COL_TILE = 512

F32 = jnp.float32
BF16 = jnp.bfloat16


def _silu(z):
    return z * jax.nn.sigmoid(z)


def _rms_scale(x):
    return lax.rsqrt(jnp.mean(x * x, axis=-1, keepdims=True) + EPS)


def _modulated_norm(x, g_norm, shift, scale):
    return x * _rms_scale(x) * g_norm * (1.0 + scale) + shift


def _layer_norm(v, g, b):
    mu = jnp.mean(v, axis=-1, keepdims=True)
    vc = v - mu
    var = jnp.mean(vc * vc, axis=-1, keepdims=True)
    return vc * lax.rsqrt(var + EPS) * g + b


def _split_proj(p):
    return [p[:, i * CONV_CH:(i + 1) * CONV_CH] for i in range(7)]


def _adaln_kernel(cp_ref, cs_ref, w_ref, b_ref, mp_ref, ms_ref):
    w = w_ref[...].astype(BF16)
    b = b_ref[...]
    ap = _silu(cp_ref[...]).astype(BF16)
    as_ = _silu(cs_ref[...]).astype(BF16)
    mp_ref[...] = jnp.dot(ap, w, preferred_element_type=F32) + b
    ms_ref[...] = jnp.dot(as_, w, preferred_element_type=F32) + b


def _adaln(c_prompt, c_sample, w_ada, b_ada):
    nb_p, nb_s = c_prompt.shape[0], c_sample.shape[0]
    n_out = w_ada.shape[1]
    return pl.pallas_call(
        _adaln_kernel,
        grid=(n_out // ADA_COL_TILE,),
        in_specs=[
            pl.BlockSpec((nb_p, D_MODEL), lambda j: (0, 0)),
            pl.BlockSpec((nb_s, D_MODEL), lambda j: (0, 0)),
            pl.BlockSpec((D_MODEL, ADA_COL_TILE), lambda j: (0, j)),
            pl.BlockSpec((1, ADA_COL_TILE), lambda j: (0, j)),
        ],
        out_specs=[
            pl.BlockSpec((nb_p, ADA_COL_TILE), lambda j: (0, j)),
            pl.BlockSpec((nb_s, ADA_COL_TILE), lambda j: (0, j)),
        ],
        out_shape=[
            jax.ShapeDtypeStruct((nb_p, n_out), F32),
            jax.ShapeDtypeStruct((nb_s, n_out), F32),
        ],
        name="adaln_mod",
    )(c_prompt, c_sample, w_ada, b_ada.reshape(1, n_out))


def _paired_tril_weights(ws_ref):
    row = lax.broadcasted_iota(jnp.int32, (CHUNK, CHUNK), 0)
    col = lax.broadcasted_iota(jnp.int32, (CHUNK, CHUNK), 1)
    causal = col <= row
    out = []
    for k in range(HEAD_PAIRS):
        w0 = jnp.where(causal, ws_ref[2 * k], 0.0)
        w1 = jnp.where(causal, ws_ref[2 * k + 1], 0.0)
        out.append(jnp.concatenate([w0, w1], axis=1).astype(BF16))
    return out


def _chunk_mix(vn, wcat, n_chunks):
    vb = vn.astype(BF16)
    lane = lax.broadcasted_iota(jnp.int32, (CHUNK, V7X_LANES), 1)
    low_half = lane < MLP_HEAD_DIM
    zero = jnp.zeros((), BF16)
    pair_out = []
    for k in range(HEAD_PAIRS):
        lo, hi = [], []
        for n in range(n_chunks):
            blk = vb[n * CHUNK:(n + 1) * CHUNK, k * V7X_LANES:(k + 1) * V7X_LANES]
            lo.append(jnp.where(low_half, blk, zero))
            hi.append(jnp.where(low_half, zero, blk))
        rhs = jnp.concatenate(
            [jnp.concatenate(lo, axis=1), jnp.concatenate(hi, axis=1)], axis=0)
        pair_out.append(jnp.dot(wcat[k], rhs, preferred_element_type=F32))
    rows = []
    for n in range(n_chunks):
        rows.append(jnp.concatenate(
            [pair_out[k][:, n * V7X_LANES:(n + 1) * V7X_LANES] for k in range(HEAD_PAIRS)], axis=1))
    return rows


def _split_mod(mod):
    return mod[:, :D_MODEL], mod[:, D_MODEL:2 * D_MODEL], mod[:, 2 * D_MODEL:]


def _prompt_kernel(x_ref, xn_ref, mod_ref, modn_ref, gn_ref, win_ref, cw_ref, gv_ref, bv_ref, ws_ref,
                   bs_ref, wout_ref, gf_ref, y_ref, cbuf_ref, spad_ref, hbuf_ref, *, tiles_per_seq):
    rows = x_ref.shape[1]
    sub = hbuf_ref.shape[0]
    n_chunks = sub // CHUNK
    t = pl.program_id(0)

    shift, scale, gate = _split_mod(mod_ref[0])
    gn = gn_ref[...]

    @pl.when(t == 0)
    def _():
        hbuf_ref[...] = _modulated_norm(x_ref[0, 0:sub, :], gn, shift, scale).astype(BF16)

    @pl.when(t % tiles_per_seq == 0)
    def _():
        spad_ref[0:V7X_SUBLANES, :] = jnp.zeros((V7X_SUBLANES, CONV_CH), F32)

    cw = cw_ref[...]
    gv, bv = gv_ref[...], bv_ref[...]
    bias = bs_ref[...]
    gf = gf_ref[...]
    wcat = _paired_tril_weights(ws_ref)

    for r0 in range(0, rows, sub):
        x = x_ref[0, r0:r0 + sub, :]
        if r0 == 0:
            h = hbuf_ref[...]
        else:
            h = _modulated_norm(x, gn, shift, scale).astype(BF16)

        def proj(k):
            return jnp.dot(h, win_ref[:, k * CONV_CH:(k + 1) * CONV_CH], preferred_element_type=F32)

        v_b, c_a, h_a, u_b, z_b, b_a, z_a = proj(5), proj(2), proj(0), proj(4), proj(6), proj(1), proj(3)

        if r0 == 0:
            n_shift, n_scale, _ = _split_mod(modn_ref[0])
            hbuf_ref[...] = _modulated_norm(xn_ref[0], gn, n_shift, n_scale).astype(BF16)

        s = c_a * h_a
        base = V7X_SUBLANES + r0
        spad_ref[base:base + sub, :] = s
        s_m1 = spad_ref[base - 1:base - 1 + sub, :]
        s_m2 = spad_ref[base - 2:base - 2 + sub, :]
        conv = cw[0:1] * s_m2 + cw[1:2] * s_m1 + cw[2:3] * s
        y_a = b_a * conv * _silu(z_a)

        vn = _layer_norm(v_b, gv, bv)
        mix_rows = _chunk_mix(vn, wcat, n_chunks)
        mix = jnp.concatenate([m + bias for m in mix_rows], axis=0)
        y_b = u_b * mix * _silu(z_b)

        y_cat = jnp.concatenate([y_a, y_b], axis=1).astype(BF16)
        y = jnp.dot(y_cat, wout_ref[...], preferred_element_type=F32)
        res = x + gate * y
        y_ref[0, r0:r0 + sub, :] = res * _rms_scale(res) * gf

    tail = spad_ref[rows:rows + V7X_SUBLANES, :]
    spad_ref[0:V7X_SUBLANES, :] = tail
    cbuf_ref[0] = tail[V7X_SUBLANES - (CONV_K - 1):, :]


def _prompt_layer(x, mod, g_norm, w_in_b, conv_w, g_v, b_v, w_s, bias_rows, w_out_b, g_final):
    nb, seq, _ = x.shape
    rows, sub = ROW_TILE, SUB_TILE
    tiles_per_seq = seq // rows
    n_steps = nb * tiles_per_seq
    subs_per_tile = rows // sub

    def cur(t):
        return t // tiles_per_seq, t % tiles_per_seq

    def nxt(t):
        return cur(jnp.minimum(t + 1, n_steps - 1))

    mod3 = mod.reshape(nb, 1, 3 * D_MODEL)
    const2 = lambda t: (0, 0)
    return pl.pallas_call(
        functools.partial(_prompt_kernel, tiles_per_seq=tiles_per_seq),
        grid=(n_steps,),
        in_specs=[
            pl.BlockSpec((1, rows, D_MODEL), lambda t: (*cur(t), 0)),
            pl.BlockSpec((1, sub, D_MODEL), lambda t: (nxt(t)[0], nxt(t)[1] * subs_per_tile, 0)),
            pl.BlockSpec((1, 1, 3 * D_MODEL), lambda t: (cur(t)[0], 0, 0)),
            pl.BlockSpec((1, 1, 3 * D_MODEL), lambda t: (nxt(t)[0], 0, 0)),
            pl.BlockSpec((1, D_MODEL), const2),
            pl.BlockSpec((D_MODEL, PROJ_WIDTH), const2),
            pl.BlockSpec((CONV_K, CONV_CH), const2),
            pl.BlockSpec((1, MLP_CH), const2),
            pl.BlockSpec((1, MLP_CH), const2),
            pl.BlockSpec((MLP_HEADS, CHUNK, CHUNK), lambda t: (0, 0, 0)),
            pl.BlockSpec((CHUNK, MLP_CH), const2),
            pl.BlockSpec((D_MODEL, D_MODEL), const2),
            pl.BlockSpec((1, D_MODEL), const2),
        ],
        out_specs=[
            pl.BlockSpec((1, rows, D_MODEL), lambda t: (*cur(t), 0)),
            pl.BlockSpec((1, CONV_K - 1, CONV_CH), lambda t: (cur(t)[0], 0, 0)),
        ],
        out_shape=[
            jax.ShapeDtypeStruct((nb, seq, D_MODEL), F32),
            jax.ShapeDtypeStruct((nb, CONV_K - 1, CONV_CH), F32),
        ],
        scratch_shapes=[
            pltpu.VMEM((rows + V7X_SUBLANES, CONV_CH), F32),
            pltpu.VMEM((sub, D_MODEL), BF16),
        ],
        compiler_params=pltpu.CompilerParams(
            dimension_semantics=("arbitrary",),
            vmem_limit_bytes=V7X_VMEM_LIMIT_BYTES),
        name="prompt_layer",
    )(x, x, mod3, mod3, g_norm.reshape(1, D_MODEL), w_in_b, conv_w,
      g_v.reshape(1, MLP_CH), b_v.reshape(1, MLP_CH), w_s, bias_rows, w_out_b,
      g_final.reshape(1, D_MODEL))


def _sample_kernel(x_ref, st_ref, mod_ref, gn_ref, win_ref, cw_ref, gv_ref, bv_ref, wv_ref, bs_ref,
                   wout_ref, gf_ref, y_ref, cbuf_ref, vn_ref):
    nb = x_ref.shape[0]
    n_tok = x_ref.shape[1] // D_MODEL
    mod = mod_ref[...]
    shift = mod[:, :D_MODEL]
    scale = mod[:, D_MODEL:2 * D_MODEL]
    gate = mod[:, 2 * D_MODEL:]
    gn = gn_ref[...]

    xs = [x_ref[:, t * D_MODEL:(t + 1) * D_MODEL] for t in range(n_tok)]
    h = jnp.concatenate([_modulated_norm(x, gn, shift, scale).astype(BF16) for x in xs], axis=0)
    p = jnp.dot(h, win_ref[...], preferred_element_type=F32)

    cw = cw_ref[...]
    gv, bv = gv_ref[...], bv_ref[...]
    wv = wv_ref[...]
    bias = bs_ref[...]
    s_pad = [st_ref[:, k * CONV_CH:(k + 1) * CONV_CH] for k in range(CONV_K - 1)]
    vns, ys = [], []
    for t in range(n_tok):
        h_a, b_a, c_a, z_a, u_b, v_b, z_b = _split_proj(p[t * nb:(t + 1) * nb])
        s_pad.append(c_a * h_a)
        conv = cw[0:1] * s_pad[t] + cw[1:2] * s_pad[t + 1] + cw[2:3] * s_pad[t + 2]
        y_a = b_a * conv * _silu(z_a)
        vns.append(_layer_norm(v_b, gv, bv))
        mix = bias[t:t + 1]
        for src in range(t + 1):
            r = t * n_tok + src
            mix = mix + wv[r:r + 1] * vns[src]
        y_b = u_b * mix * _silu(z_b)
        ys.append(jnp.concatenate([y_a, y_b], axis=1).astype(BF16))

    y = jnp.dot(jnp.concatenate(ys, axis=0), wout_ref[...], preferred_element_type=F32)
    gf = gf_ref[...]
    for t in range(n_tok):
        res = xs[t] + gate * y[t * nb:(t + 1) * nb]
        y_ref[:, t * D_MODEL:(t + 1) * D_MODEL] = res * _rms_scale(res) * gf
        vn_ref[:, t * MLP_CH:(t + 1) * MLP_CH] = vns[t]
    for k in range(CONV_K - 1):
        cbuf_ref[:, k * CONV_CH:(k + 1) * CONV_CH] = s_pad[n_tok + k]


def _sample_layer(x2d, st2d, mod, g_norm, w_in_b, conv_w, g_v, b_v, wv_rows, bias_rows, w_out_b,
                  g_final):
    nb = x2d.shape[0]
    n_tok = x2d.shape[1] // D_MODEL
    full = lambda a: pl.BlockSpec(a.shape, lambda i: (0,) * a.ndim)
    args = (x2d, st2d, mod, g_norm.reshape(1, D_MODEL), w_in_b, conv_w, g_v.reshape(1, MLP_CH),
            b_v.reshape(1, MLP_CH), wv_rows, bias_rows, w_out_b, g_final.reshape(1, D_MODEL))
    out_shape = [
        jax.ShapeDtypeStruct((nb, n_tok * D_MODEL), F32),
        jax.ShapeDtypeStruct((nb, (CONV_K - 1) * CONV_CH), F32),
        jax.ShapeDtypeStruct((nb, n_tok * MLP_CH), F32),
    ]
    return pl.pallas_call(
        _sample_kernel,
        grid=(1,),
        in_specs=[full(a) for a in args],
        out_specs=[pl.BlockSpec(s.shape, lambda i: (0, 0)) for s in out_shape],
        out_shape=out_shape,
        compiler_params=pltpu.CompilerParams(
            dimension_semantics=("arbitrary",),
            vmem_limit_bytes=V7X_VMEM_LIMIT_BYTES),
        name="sample_layer",
    )(*args)


def kernel(x_prompt, x_sample, state_conv, c_prompt, c_sample, w_ada, b_ada, g_norm, w_in, conv_w,
           g_v, b_v, w_s, b_s, w_out, g_final):
    depth = w_ada.shape[0]
    assert depth == 1, "single-layer trunk"
    nb_s, n_tok, _ = x_sample.shape
    assert n_tok <= CHUNK

    mod_p, mod_s = _adaln(c_prompt, c_sample, w_ada[0], b_ada[0])
    w_in_b = w_in[0].astype(BF16)
    w_out_b = w_out[0].astype(BF16)
    bias_rows = jnp.repeat(b_s[0].T, MLP_HEAD_DIM, axis=1)
    wv_rows = jnp.repeat(
        w_s[0][:, :n_tok, :n_tok].transpose(1, 2, 0).reshape(n_tok * n_tok, MLP_HEADS),
        MLP_HEAD_DIM, axis=1)

    y_prompt, conv_p = _prompt_layer(x_prompt, mod_p, g_norm[0], w_in_b, conv_w[0], g_v[0], b_v[0],
                                     w_s[0], bias_rows, w_out_b, g_final)
    y_s, conv_s, vn_s = _sample_layer(
        x_sample.reshape(nb_s, n_tok * D_MODEL),
        state_conv[0].reshape(nb_s, (CONV_K - 1) * CONV_CH),
        mod_s, g_norm[0], w_in_b, conv_w[0], g_v[0], b_v[0], wv_rows, bias_rows, w_out_b, g_final)

    return (y_prompt,
            y_s.reshape(nb_s, n_tok, D_MODEL),
            conv_p[None],
            conv_s.reshape(1, nb_s, CONV_K - 1, CONV_CH),
            vn_s.reshape(1, nb_s, n_tok, MLP_CH))
```

```python
import functools

import jax
import jax.numpy as jnp
from jax import lax
from jax.experimental import pallas as pl
from jax.experimental.pallas import tpu as pltpu

D_MODEL = 1024
CONV_CH = 512
MLP_CH = 512
MLP_HEADS = 8
MLP_HEAD_DIM = 64
CONV_K = 3
CHUNK = 128
EPS = 1e-6
PROJ_WIDTH = 4 * CONV_CH + 3 * MLP_CH
HEAD_PAIRS = MLP_HEADS // 2

V7X_LANES = 128
V7X_SUBLANES = 8
V7X_VMEM_LIMIT_BYTES = 56 * 1024 * 1024

ROW_TILE = 512
SUB_TILE = 256
ADA_COL_TILE = 512

F32 = jnp.float32
BF16 = jnp.bfloat16


def _silu(z):
    return z * jax.nn.sigmoid(z)


def _rms_scale(x):
    return lax.rsqrt(jnp.mean(x * x, axis=-1, keepdims=True) + EPS)


def _modulated_norm(x, g_norm, shift, scale):
    return x * _rms_scale(x) * g_norm * (1.0 + scale) + shift


def _layer_norm(v, g, b):
    mu = jnp.mean(v, axis=-1, keepdims=True)
    vc = v - mu
    var = jnp.mean(vc * vc, axis=-1, keepdims=True)
    return vc * lax.rsqrt(var + EPS) * g + b


def _split_proj(p):
    return [p[:, i * CONV_CH:(i + 1) * CONV_CH] for i in range(7)]


def _adaln_kernel(cp_ref, cs_ref, w_ref, b_ref, mp_ref, ms_ref):
    w = w_ref[...].astype(BF16)
    b = b_ref[...]
    ap = _silu(cp_ref[...]).astype(BF16)
    as_ = _silu(cs_ref[...]).astype(BF16)
    mp_ref[...] = jnp.dot(ap, w, preferred_element_type=F32) + b
    ms_ref[...] = jnp.dot(as_, w, preferred_element_type=F32) + b


def _adaln(c_prompt, c_sample, w_ada, b_ada):
    nb_p, nb_s = c_prompt.shape[0], c_sample.shape[0]
    n_out = w_ada.shape[1]
    return pl.pallas_call(
        _adaln_kernel,
        grid=(n_out // ADA_COL_TILE,),
        in_specs=[
            pl.BlockSpec((nb_p, D_MODEL), lambda j: (0, 0)),
            pl.BlockSpec((nb_s, D_MODEL), lambda j: (0, 0)),
            pl.BlockSpec((D_MODEL, ADA_COL_TILE), lambda j: (0, j)),
            pl.BlockSpec((1, ADA_COL_TILE), lambda j: (0, j)),
        ],
        out_specs=[
            pl.BlockSpec((nb_p, ADA_COL_TILE), lambda j: (0, j)),
            pl.BlockSpec((nb_s, ADA_COL_TILE), lambda j: (0, j)),
        ],
        out_shape=[
            jax.ShapeDtypeStruct((nb_p, n_out), F32),
            jax.ShapeDtypeStruct((nb_s, n_out), F32),
        ],
        name="adaln_mod",
    )(c_prompt, c_sample, w_ada, b_ada.reshape(1, n_out))


def _paired_tril_weights(ws_ref):
    row = lax.broadcasted_iota(jnp.int32, (CHUNK, CHUNK), 0)
    col = lax.broadcasted_iota(jnp.int32, (CHUNK, CHUNK), 1)
    causal = col <= row
    out = []
    for k in range(HEAD_PAIRS):
        w0 = jnp.where(causal, ws_ref[2 * k], 0.0)
        w1 = jnp.where(causal, ws_ref[2 * k + 1], 0.0)
        out.append(jnp.concatenate([w0, w1], axis=1).astype(BF16))
    return out


def _chunk_mix(vn, wcat, n_chunks):
    vb = vn.astype(BF16)
    lane = lax.broadcasted_iota(jnp.int32, (CHUNK, V7X_LANES), 1)
    low_half = lane < MLP_HEAD_DIM
    zero = jnp.zeros((), BF16)
    pair_out = []
    for k in range(HEAD_PAIRS):
        lo, hi = [], []
        for n in range(n_chunks):
            blk = vb[n * CHUNK:(n + 1) * CHUNK, k * V7X_LANES:(k + 1) * V7X_LANES]
            lo.append(jnp.where(low_half, blk, zero))
            hi.append(jnp.where(low_half, zero, blk))
        rhs = jnp.concatenate(
            [jnp.concatenate(lo, axis=1), jnp.concatenate(hi, axis=1)], axis=0)
        pair_out.append(jnp.dot(wcat[k], rhs, preferred_element_type=F32))
    rows = []
    for n in range(n_chunks):
        rows.append(jnp.concatenate(
            [pair_out[k][:, n * V7X_LANES:(n + 1) * V7X_LANES] for k in range(HEAD_PAIRS)], axis=1))
    return rows


def _split_mod(mod):
    return mod[:, :D_MODEL], mod[:, D_MODEL:2 * D_MODEL], mod[:, 2 * D_MODEL:]


def _prompt_kernel(x_ref, xn_ref, mod_ref, modn_ref, gn_ref, win_ref, cw_ref, gv_ref, bv_ref, ws_ref,
                   bs_ref, wout_ref, gf_ref, y_ref, cbuf_ref, spad_ref, hbuf_ref, *, tiles_per_seq):
    rows = x_ref.shape[1]
    sub = hbuf_ref.shape[0]
    n_chunks = sub // CHUNK
    t = pl.program_id(0)

    shift, scale, gate = _split_mod(mod_ref[0])
    gn = gn_ref[...]

    @pl.when(t == 0)
    def _():
        hbuf_ref[...] = _modulated_norm(x_ref[0, 0:sub, :], gn, shift, scale).astype(BF16)

    @pl.when(t % tiles_per_seq == 0)
    def _():
        spad_ref[0:V7X_SUBLANES, :] = jnp.zeros((V7X_SUBLANES, CONV_CH), F32)

    cw = cw_ref[...]
    gv, bv = gv_ref[...], bv_ref[...]
    bias = bs_ref[...]
    gf = gf_ref[...]
    wcat = _paired_tril_weights(ws_ref)

    for r0 in range(0, rows, sub):
        x = x_ref[0, r0:r0 + sub, :]
        if r0 == 0:
            h = hbuf_ref[...]
        else:
            h = _modulated_norm(x, gn, shift, scale).astype(BF16)

        def proj(k):
            return jnp.dot(h, win_ref[:, k * CONV_CH:(k + 1) * CONV_CH], preferred_element_type=F32)

        v_b, c_a, h_a, u_b, z_b, b_a, z_a = proj(5), proj(2), proj(0), proj(4), proj(6), proj(1), proj(3)

        if r0 == 0:
            n_shift, n_scale, _ = _split_mod(modn_ref[0])
            hbuf_ref[...] = _modulated_norm(xn_ref[0], gn, n_shift, n_scale).astype(BF16)

        s = c_a * h_a
        base = V7X_SUBLANES + r0
        spad_ref[base:base + sub, :] = s
        s_m1 = spad_ref[base - 1:base - 1 + sub, :]
        s_m2 = spad_ref[base - 2:base - 2 + sub, :]
        conv = cw[0:1] * s_m2 + cw[1:2] * s_m1 + cw[2:3] * s
        y_a = b_a * conv * _silu(z_a)

        vn = _layer_norm(v_b, gv, bv)
        mix_rows = _chunk_mix(vn, wcat, n_chunks)
        mix = jnp.concatenate([m + bias for m in mix_rows], axis=0)
        y_b = u_b * mix * _silu(z_b)

        y_cat = jnp.concatenate([y_a, y_b], axis=1).astype(BF16)
        y = jnp.dot(y_cat, wout_ref[...], preferred_element_type=F32)
        res = x + gate * y
        y_ref[0, r0:r0 + sub, :] = res * _rms_scale(res) * gf

    tail = spad_ref[rows:rows + V7X_SUBLANES, :]
    spad_ref[0:V7X_SUBLANES, :] = tail
    cbuf_ref[0] = tail[V7X_SUBLANES - (CONV_K - 1):, :]


def _prompt_layer(x, mod, g_norm, w_in_b, conv_w, g_v, b_v, w_s, bias_rows, w_out_b, g_final):
    nb, seq, _ = x.shape
    rows, sub = ROW_TILE, SUB_TILE
    tiles_per_seq = seq // rows
    n_steps = nb * tiles_per_seq
    subs_per_tile = rows // sub

    def cur(t):
        return t // tiles_per_seq, t % tiles_per_seq

    def nxt(t):
        return cur(jnp.minimum(t + 1, n_steps - 1))

    mod3 = mod.reshape(nb, 1, 3 * D_MODEL)
    const2 = lambda t: (0, 0)
    return pl.pallas_call(
        functools.partial(_prompt_kernel, tiles_per_seq=tiles_per_seq),
        grid=(n_steps,),
        in_specs=[
            pl.BlockSpec((1, rows, D_MODEL), lambda t: (*cur(t), 0)),
            pl.BlockSpec((1, sub, D_MODEL), lambda t: (nxt(t)[0], nxt(t)[1] * subs_per_tile, 0)),
            pl.BlockSpec((1, 1, 3 * D_MODEL), lambda t: (cur(t)[0], 0, 0)),
            pl.BlockSpec((1, 1, 3 * D_MODEL), lambda t: (nxt(t)[0], 0, 0)),
            pl.BlockSpec((1, D_MODEL), const2),
            pl.BlockSpec((D_MODEL, PROJ_WIDTH), const2),
            pl.BlockSpec((CONV_K, CONV_CH), const2),
            pl.BlockSpec((1, MLP_CH), const2),
            pl.BlockSpec((1, MLP_CH), const2),
            pl.BlockSpec((MLP_HEADS, CHUNK, CHUNK), lambda t: (0, 0, 0)),
            pl.BlockSpec((CHUNK, MLP_CH), const2),
            pl.BlockSpec((D_MODEL, D_MODEL), const2),
            pl.BlockSpec((1, D_MODEL), const2),
        ],
        out_specs=[
            pl.BlockSpec((1, rows, D_MODEL), lambda t: (*cur(t), 0)),
            pl.BlockSpec((1, CONV_K - 1, CONV_CH), lambda t: (cur(t)[0], 0, 0)),
        ],
        out_shape=[
            jax.ShapeDtypeStruct((nb, seq, D_MODEL), F32),
            jax.ShapeDtypeStruct((nb, CONV_K - 1, CONV_CH), F32),
        ],
        scratch_shapes=[
            pltpu.VMEM((rows + V7X_SUBLANES, CONV_CH), F32),
            pltpu.VMEM((sub, D_MODEL), BF16),
        ],
        compiler_params=pltpu.CompilerParams(
            dimension_semantics=("arbitrary",),
            vmem_limit_bytes=V7X_VMEM_LIMIT_BYTES),
        name="prompt_layer",
    )(x, x, mod3, mod3, g_norm.reshape(1, D_MODEL), w_in_b, conv_w,
      g_v.reshape(1, MLP_CH), b_v.reshape(1, MLP_CH), w_s, bias_rows, w_out_b,
      g_final.reshape(1, D_MODEL))


def _sample_kernel(x_hbm, st_hbm, mod_ref, gn_ref, win_ref, cw_ref, gv_ref, bv_ref, wv_ref, bs_ref,
                   wout_ref, gf_ref, y_hbm, cbuf_hbm, vn_hbm,
                   x_buf, st_buf, y_buf, cb_buf, vn_buf, in_sem, out_sem):
    n_tok, nb, _ = x_buf.shape
    n_state = st_buf.shape[0]

    loads = [pltpu.make_async_copy(x_hbm.at[:, t, :], x_buf.at[t], in_sem.at[t])
             for t in range(n_tok)]
    loads += [pltpu.make_async_copy(st_hbm.at[:, k, :], st_buf.at[k], in_sem.at[n_tok + k])
              for k in range(n_state)]
    stores = [pltpu.make_async_copy(y_buf.at[t], y_hbm.at[:, t, :], out_sem.at[t])
              for t in range(n_tok)]
    stores += [pltpu.make_async_copy(vn_buf.at[t], vn_hbm.at[:, t, :], out_sem.at[n_tok + t])
               for t in range(n_tok)]
    stores += [pltpu.make_async_copy(cb_buf.at[k], cbuf_hbm.at[:, k, :],
                                     out_sem.at[2 * n_tok + k]) for k in range(n_state)]
    for c in loads:
        c.start()

    shift, scale, gate = _split_mod(mod_ref[...])
    gn = gn_ref[...]
    for c in loads:
        c.wait()

    h = jnp.concatenate(
        [_modulated_norm(x_buf[t], gn, shift, scale).astype(BF16) for t in range(n_tok)], axis=0)
    p = jnp.dot(h, win_ref[...], preferred_element_type=F32)

    cw = cw_ref[...]
    gv, bv = gv_ref[...], bv_ref[...]
    wv = wv_ref[...]
    bias = bs_ref[...]
    s_pad = [st_buf[k] for k in range(n_state)]
    vns, ys = [], []
    for t in range(n_tok):
        h_a, b_a, c_a, z_a, u_b, v_b, z_b = _split_proj(p[t * nb:(t + 1) * nb])
        s_pad.append(c_a * h_a)
        conv = cw[0:1] * s_pad[t] + cw[1:2] * s_pad[t + 1] + cw[2:3] * s_pad[t + 2]
        y_a = b_a * conv * _silu(z_a)
        vns.append(_layer_norm(v_b, gv, bv))
        vn_buf[t] = vns[t]
        mix = bias[t:t + 1]
        for src in range(t + 1):
            r = t * n_tok + src
            mix = mix + wv[r:r + 1] * vns[src]
        y_b = u_b * mix * _silu(z_b)
        ys.append(jnp.concatenate([y_a, y_b], axis=1).astype(BF16))
    for k in range(n_state):
        cb_buf[k] = s_pad[n_tok + k]

    y = jnp.dot(jnp.concatenate(ys, axis=0), wout_ref[...], preferred_element_type=F32)
    gf = gf_ref[...]
    for t in range(n_tok):
        res = x_buf[t] + gate * y[t * nb:(t + 1) * nb]
        y_buf[t] = res * _rms_scale(res) * gf

    for c in stores:
        c.start()
    for c in stores:
        c.wait()


def _sample_layer(x, state, mod, g_norm, w_in_b, conv_w, g_v, b_v, wv_rows, bias_rows, w_out_b,
                  g_final):
    nb, n_tok, _ = x.shape
    n_state = state.shape[1]
    full = lambda a: pl.BlockSpec(a.shape, lambda i: (0,) * len(a.shape))
    hbm = pl.BlockSpec(memory_space=pl.ANY)
    vmem_args = (mod, g_norm.reshape(1, D_MODEL), w_in_b, conv_w, g_v.reshape(1, MLP_CH),
                 b_v.reshape(1, MLP_CH), wv_rows, bias_rows, w_out_b, g_final.reshape(1, D_MODEL))
    out_shape = [
        jax.ShapeDtypeStruct((nb, n_tok, D_MODEL), F32),
        jax.ShapeDtypeStruct((nb, n_state, CONV_CH), F32),
        jax.ShapeDtypeStruct((nb, n_tok, MLP_CH), F32),
    ]
    return pl.pallas_call(
        _sample_kernel,
        grid=(1,),
        in_specs=[hbm, hbm] + [full(a) for a in vmem_args],
        out_specs=[hbm, hbm, hbm],
        out_shape=out_shape,
        scratch_shapes=[
            pltpu.VMEM((n_tok, nb, D_MODEL), F32),
            pltpu.VMEM((n_state, nb, CONV_CH), F32),
            pltpu.VMEM((n_tok, nb, D_MODEL), F32),
            pltpu.VMEM((n_state, nb, CONV_CH), F32),
            pltpu.VMEM((n_tok, nb, MLP_CH), F32),
            pltpu.SemaphoreType.DMA((n_tok + n_state,)),
            pltpu.SemaphoreType.DMA((2 * n_tok + n_state,)),
        ],
        compiler_params=pltpu.CompilerParams(
            dimension_semantics=("arbitrary",),
            vmem_limit_bytes=V7X_VMEM_LIMIT_BYTES),
        name="sample_layer",
    )(x, state, *vmem_args)


def kernel(x_prompt, x_sample, state_conv, c_prompt, c_sample, w_ada, b_ada, g_norm, w_in, conv_w,
           g_v, b_v, w_s, b_s, w_out, g_final):
    depth = w_ada.shape[0]
    assert depth == 1, "single-layer trunk"
    nb_s, n_tok, _ = x_sample.shape
    assert n_tok <= CHUNK

    mod_p, mod_s = _adaln(c_prompt, c_sample, w_ada[0], b_ada[0])
    w_in_b = w_in[0].astype(BF16)
    w_out_b = w_out[0].astype(BF16)
    bias_rows = jnp.repeat(b_s[0].T, MLP_HEAD_DIM, axis=1)
    wv_rows = jnp.repeat(
        w_s[0][:, :n_tok, :n_tok].transpose(1, 2, 0).reshape(n_tok * n_tok, MLP_HEADS),
        MLP_HEAD_DIM, axis=1)

    y_prompt, conv_p = _prompt_layer(x_prompt, mod_p, g_norm[0], w_in_b, conv_w[0], g_v[0], b_v[0],
                                     w_s[0], bias_rows, w_out_b, g_final)
    y_s, conv_s, vn_s = _sample_layer(
        x_sample, state_conv[0], mod_s, g_norm[0], w_in_b, conv_w[0], g_v[0], b_v[0], wv_rows,
        bias_rows, w_out_b, g_final)

    return (y_prompt, y_s, conv_p[None], conv_s[None], vn_s[None])
```
